```python
import jax, jax.numpy as jnp
from jax import lax
import numpy as np

D_MODEL = 1024
BATCH = 8
SEQ = 4096
DEPTH = 4

N_MIXERS = 3
GRID_W = 64
FFN_HIDDEN = -(-8 * D_MODEL // (3 * 256)) * 256
RMS_EPS = 1e-6
LRU_WIDTH = D_MODEL
LRU_HEADS = 8
LRU_BLOCK = LRU_WIDTH // LRU_HEADS
LRU_CONV_W = 4
LRU_C = 8.0
NA_HEADS = 16
NA_HEAD_DIM = D_MODEL // NA_HEADS
NA_WIN_ROWS = 8
NA_WIN_COLS = 16
NEG_INF = -1e30
SC_CONV_W = 3

kernel_name = 'hybrid_lru_natten_shortconv_encoder'


def rms_norm(x, g):
    x32 = x.astype(jnp.float32)
    y = x32 * lax.rsqrt(jnp.mean(x32 * x32, axis=-1, keepdims=True) + RMS_EPS)
    return y.astype(x.dtype) * g


def modulate(h, shift, scale):
    return h * (1 + scale[:, None, :]) + shift[:, None, :]


def depthwise_conv(x, w, pad_left):
    k, ch = w.shape
    return lax.conv_general_dilated(
        x, w[:, None, :], window_strides=(1,),
        padding=[(pad_left, k - 1 - pad_left)],
        dimension_numbers=('NWC', 'WIO', 'NWC'),
        feature_group_count=ch)


def _linear_combine(left, right):
    a_l, b_l = left
    a_r, b_r = right
    return a_l * a_r, a_r * b_l + b_r


def rglru_block(h, w_in, conv_w, conv_b, gate_w, gate_b, lam, w_out):
    bsz, t, _ = h.shape
    y_br, x_br = jnp.split(h @ w_in, 2, axis=-1)
    xc = depthwise_conv(x_br, conv_w, LRU_CONV_W // 2) + conv_b
    xh = xc.reshape(bsz, t, LRU_HEADS, LRU_BLOCK)
    gates = jnp.einsum('btnk,dgnkj->dgbtnj', xh, gate_w).reshape(2, 2, bsz, t, LRU_WIDTH)
    gates = jax.nn.sigmoid((gates + gate_b[:, :, None, None, :]).astype(jnp.float32))
    r, i = gates[:, 0], gates[:, 1]
    log_a = -LRU_C * r * jax.nn.softplus(-lam.astype(jnp.float32))[:, None, None, :]
    a = jnp.exp(log_a)
    b = jnp.sqrt(-jnp.expm1(2.0 * log_a)) * i * xc.astype(jnp.float32)[None]
    _, h_fwd = lax.associative_scan(_linear_combine, (a[0], b[0]), axis=1)
    _, h_bwd = lax.associative_scan(_linear_combine, (a[1], b[1]), axis=1, reverse=True)
    rec = (h_fwd + h_bwd).astype(h.dtype)
    return (rec * jax.nn.gelu(y_br)) @ w_out


def neighbourhood_attention(h, w_qkv, rpb, w_o):
    bsz, t, d = h.shape
    rows = t // GRID_W
    kh = min(NA_WIN_ROWS, rows)
    qkv = (h @ w_qkv).reshape(bsz, rows, GRID_W, 3, NA_HEADS, NA_HEAD_DIM)
    q, k, v = qkv[:, :, :, 0], qkv[:, :, :, 1], qkv[:, :, :, 2]
    row = jnp.arange(rows)
    key_rows = jnp.clip(row - kh // 2, 0, rows - kh)[:, None] + jnp.arange(kh)[None, :]
    k_band = k[:, key_rows].reshape(bsz, rows, kh * GRID_W, NA_HEADS, NA_HEAD_DIM)
    v_band = v[:, key_rows].reshape(bsz, rows, kh * GRID_W, NA_HEADS, NA_HEAD_DIM)
    col = jnp.arange(GRID_W)
    col_start = jnp.clip(col - NA_WIN_COLS // 2, 0, GRID_W - NA_WIN_COLS)
    col_in = (col[None, :] >= col_start[:, None]) & (col[None, :] < col_start[:, None] + NA_WIN_COLS)
    dy = key_rows - row[:, None] + (NA_WIN_ROWS - 1)
    dx = jnp.clip(col[None, :] - col[:, None], -(NA_WIN_COLS - 1), NA_WIN_COLS - 1) + (NA_WIN_COLS - 1)
    bias = rpb[:, dy[:, None, :, None], dx[None, :, None, :]].astype(jnp.float32)
    bias = jnp.where(col_in[None, None, :, None, :], bias, NEG_INF)
    bias = bias.reshape(NA_HEADS, rows, GRID_W, kh * GRID_W)
    scores = jnp.einsum('brqhd,brkhd->bhrqk', q, k_band).astype(jnp.float32) * (NA_HEAD_DIM ** -0.5) + bias[None]
    p = jax.nn.softmax(scores, axis=-1).astype(v.dtype)
    o = jnp.einsum('bhrqk,brkhd->brqhd', p, v_band).reshape(bsz, t, d)
    return o @ w_o


def short_conv_mixer(h, w_in, conv_w, w_out):
    b_gate, c_gate, hx = jnp.split(h @ w_in, 3, axis=-1)
    y = b_gate * depthwise_conv(c_gate * hx, conv_w, SC_CONV_W // 2)
    return y @ w_out


def swiglu(h, w_gu, w_down):
    g, u = jnp.split(h @ w_gu, 2, axis=-1)
    return (jax.nn.silu(g) * u) @ w_down


def setup_inputs(seed: int = 0) -> dict:
    key = jax.random.key(seed)
    ks = jax.random.split(key, 20)
    n_a = (DEPTH + 2) // 3
    n_b = (DEPTH + 1) // 3
    n_c = DEPTH // 3
    f32 = jnp.float32

    def nrm(k, shape, s):
        return jax.random.normal(k, shape, f32) * s

    u = jax.random.uniform(ks[12], (n_a, 2, LRU_WIDTH), f32, 0.9, 0.999)
    p = u ** (1.0 / LRU_C)
    lru_lambda = jnp.log(p) - jnp.log1p(-p)
    return {
        'x': nrm(ks[0], (BATCH, SEQ, D_MODEL), 1.0),
        'c': nrm(ks[1], (BATCH, D_MODEL), 1.0),
        'ada_w': nrm(ks[2], (DEPTH, D_MODEL, 6 * D_MODEL), D_MODEL ** -0.5),
        'ada_b': nrm(ks[3], (DEPTH, 6 * D_MODEL), 0.01),
        'norm_g': 1.0 + nrm(ks[4], (DEPTH, 4, D_MODEL), 0.05),
        'ffn_w_gu': nrm(ks[5], (DEPTH, D_MODEL, 2 * FFN_HIDDEN), D_MODEL ** -0.5),
        'ffn_w_down': nrm(ks[6], (DEPTH, FFN_HIDDEN, D_MODEL), FFN_HIDDEN ** -0.5),
        'lru_w_in': nrm(ks[7], (n_a, D_MODEL, 2 * LRU_WIDTH), D_MODEL ** -0.5),
        'lru_conv_w': nrm(ks[8], (n_a, LRU_CONV_W, LRU_WIDTH), LRU_CONV_W ** -0.5),
        'lru_conv_b': nrm(ks[9], (n_a, LRU_WIDTH), 0.01),
        'lru_gate_w': nrm(ks[10], (n_a, 2, 2, LRU_HEADS, LRU_BLOCK, LRU_BLOCK), LRU_BLOCK ** -0.5),
        'lru_gate_b': nrm(ks[11], (n_a, 2, 2, LRU_WIDTH), 0.01),
        'lru_lambda': lru_lambda,
        'lru_w_out': nrm(ks[13], (n_a, LRU_WIDTH, D_MODEL), LRU_WIDTH ** -0.5),
        'na_w_qkv': nrm(ks[14], (n_b, D_MODEL, 3 * D_MODEL), D_MODEL ** -0.5),
        'na_rpb': nrm(ks[15], (n_b, NA_HEADS, 2 * NA_WIN_ROWS - 1, 2 * NA_WIN_COLS - 1), 0.02),
        'na_w_o': nrm(ks[16], (n_b, D_MODEL, D_MODEL), D_MODEL ** -0.5),
        'sc_w_in': nrm(ks[17], (n_c, D_MODEL, 3 * D_MODEL), D_MODEL ** -0.5),
        'sc_conv_w': nrm(ks[18], (n_c, SC_CONV_W, D_MODEL), SC_CONV_W ** -0.5),
        'sc_w_out': nrm(ks[19], (n_c, D_MODEL, D_MODEL), D_MODEL ** -0.5),
    }


def reference(x, c, ada_w, ada_b, norm_g, ffn_w_gu, ffn_w_down,
              lru_w_in, lru_conv_w, lru_conv_b, lru_gate_w, lru_gate_b, lru_lambda, lru_w_out,
              na_w_qkv, na_rpb, na_w_o, sc_w_in, sc_conv_w, sc_w_out):
    c_act = jax.nn.silu(c)
    for i in range(DEPTH):
        mod = c_act @ ada_w[i] + ada_b[i]
        sh_m, sc_m, g_m, sh_f, sc_f, g_f = jnp.split(mod, 6, axis=-1)
        h = modulate(rms_norm(x, norm_g[i, 0]), sh_m, sc_m)
        kind, j = i % N_MIXERS, i // N_MIXERS
        if kind == 0:
            m = rglru_block(h, lru_w_in[j], lru_conv_w[j], lru_conv_b[j], lru_gate_w[j],
                            lru_gate_b[j], lru_lambda[j], lru_w_out[j])
        elif kind == 1:
            m = neighbourhood_attention(h, na_w_qkv[j], na_rpb[j], na_w_o[j])
        else:
            m = short_conv_mixer(h, sc_w_in[j], sc_conv_w[j], sc_w_out[j])
        x = x + g_m[:, None, :] * rms_norm(m, norm_g[i, 1])
        h = modulate(rms_norm(x, norm_g[i, 2]), sh_f, sc_f)
        f = swiglu(h, ffn_w_gu[i], ffn_w_down[i])
        x = x + g_f[:, None, :] * rms_norm(f, norm_g[i, 3])
    return x
```

```python
import functools

import jax
import jax.numpy as jnp
from jax import lax
from jax.experimental import pallas as pl
from jax.experimental.pallas import tpu as pltpu

F32 = jnp.float32
BF16 = jnp.bfloat16

RMS_EPS = 1e-6
GRID_W = 64
NA_HEADS = 16
NA_WIN_ROWS = 8
NA_WIN_COLS = 16
NEG_INF = -1e30
LRU_HEADS = 8
LRU_C = 8.0

V7X_VMEM_LIMIT_BYTES = 56 * 1024 * 1024
HALO = 8
ROW_TILE = 512


def _params(*sem):
    return pltpu.CompilerParams(dimension_semantics=sem, vmem_limit_bytes=V7X_VMEM_LIMIT_BYTES)


def _resident(shape):
    nd = len(shape)
    return pl.BlockSpec(shape, lambda *_: (0,) * nd, pipeline_mode=pl.Buffered(1))


def _rms(x, g):
    ms = jnp.mean(x * x, axis=-1, keepdims=True)
    return x * lax.rsqrt(ms + RMS_EPS) * g


def _modnorm(x, g, scale, shift):
    return _rms(x, g) * (1.0 + scale) + shift


def _dot(a, b):
    return jnp.dot(a, b, preferred_element_type=F32)


def _sigmoid(x):
    return 1.0 / (1.0 + jnp.exp(-x))


def _gelu_tanh(x):
    return 0.5 * x * (1.0 + jnp.tanh(0.7978845608028654 * (x + 0.044715 * (x * x * x))))


def _ada_kernel(c_ref, w_ref, b_ref, o_ref):
    c = c_ref[...]
    ca = (c * _sigmoid(c)).astype(BF16)
    o_ref[0] = _dot(ca, w_ref[0].astype(BF16)) + b_ref[0]


def _ada_mod(c, ada_w, ada_b):
    depth, d, n = ada_w.shape
    bsz = c.shape[0]
    tn = 2048
    return pl.pallas_call(
        _ada_kernel,
        grid=(depth, n // tn),
        in_specs=[pl.BlockSpec((bsz, d), lambda l, j: (0, 0)),
                  pl.BlockSpec((1, d, tn), lambda l, j: (l, 0, j)),
                  pl.BlockSpec((1, 1, tn), lambda l, j: (l, 0, j))],
        out_specs=pl.BlockSpec((1, bsz, tn), lambda l, j: (l, 0, j)),
        out_shape=jax.ShapeDtypeStruct((depth, bsz, n), F32),
        compiler_params=_params("parallel", "parallel"),
        name="ada_mod",
    )(c, ada_w, ada_b.reshape(depth, 1, n))


def _ffn_kernel(x_ref, mod_ref, ng_ref, wgu_ref, wd_ref, o_ref, acc_ref, *, hidden, hc):
    x = x_ref[0]
    mod = mod_ref[0]
    ng = ng_ref[...]
    h = _modnorm(x, ng[2:3], mod[4:5], mod[3:4]).astype(BF16)
    for j in range(hidden // hc):
        g = _dot(h, wgu_ref[:, j * hc:(j + 1) * hc])
        u = _dot(h, wgu_ref[:, hidden + j * hc:hidden + (j + 1) * hc])
        a = (g * _sigmoid(g) * u).astype(BF16)
        p = _dot(a, wd_ref[j * hc:(j + 1) * hc, :])
        if j == 0:
            acc_ref[...] = p
        else:
            acc_ref[...] += p
    o_ref[0] = x + mod[5:6] * _rms(acc_ref[...], ng[3:4])


def _ffn_layer(x, mod, ng, w_gu, w_down):
    bsz, t, d = x.shape
    hidden = w_down.shape[0]
    tm = min(ROW_TILE, t)
    kern = functools.partial(_ffn_kernel, hidden=hidden, hc=256)
    return pl.pallas_call(
        kern,
        grid=(bsz, t // tm),
        in_specs=[pl.BlockSpec((1, tm, d), lambda b, i: (b, i, 0)),
                  pl.BlockSpec((1, 6, d), lambda b, i: (b, 0, 0)),
                  _resident(ng.shape), _resident(w_gu.shape), _resident(w_down.shape)],
        out_specs=pl.BlockSpec((1, tm, d), lambda b, i: (b, i, 0)),
        out_shape=jax.ShapeDtypeStruct(x.shape, F32),
        scratch_shapes=[pltpu.VMEM((tm, d), F32)],
        compiler_params=_params("parallel", "parallel"),
        name="ffn",
    )(x, mod, ng, w_gu, w_down)


def _halo_specs(tm, t, d):
    r = tm // HALO
    last = t // HALO - 1
    prev = pl.BlockSpec((1, HALO, d), lambda b, i: (b, jnp.maximum(i * r - 1, 0), 0))
    nxt = pl.BlockSpec((1, HALO, d), lambda b, i: (b, jnp.minimum((i + 1) * r, last), 0))
    return prev, nxt


def _sc_kernel(x_ref, xp_ref, xn_ref, mod_ref, ng_ref, win_ref, cw_ref, wout_ref, o_ref,
               acc_ref, ext_ref, *, cc):
    i = pl.program_id(1)
    nt = pl.num_programs(1)
    tm, d = acc_ref.shape
    x = x_ref[0]
    mod = mod_ref[0]
    ng = ng_ref[...]
    g0, sc, sh = ng[0:1], mod[1:2], mod[0:1]
    h = _modnorm(x, g0, sc, sh).astype(BF16)
    he = jnp.concatenate([_modnorm(xp_ref[0], g0, sc, sh), _modnorm(xn_ref[0], g0, sc, sh)],
                         axis=0).astype(BF16)
    keep_prev = (i > 0).astype(F32)
    keep_next = (i < nt - 1).astype(F32)
    cw = cw_ref[...]
    for j in range(d // cc):
        lo, hi = j * cc, (j + 1) * cc
        bg = _dot(h, win_ref[:, lo:hi])
        v = _dot(h, win_ref[:, d + lo:d + hi]) * _dot(h, win_ref[:, 2 * d + lo:2 * d + hi])
        ve = _dot(he, win_ref[:, d + lo:d + hi]) * _dot(he, win_ref[:, 2 * d + lo:2 * d + hi])
        ext_ref[0:HALO, :] = ve[0:HALO] * keep_prev
        ext_ref[HALO:HALO + tm, :] = v
        ext_ref[HALO + tm:, :] = ve[HALO:] * keep_next
        conv = (cw[0:1, lo:hi] * ext_ref[HALO - 1:HALO - 1 + tm, :]
                + cw[1:2, lo:hi] * v
                + cw[2:3, lo:hi] * ext_ref[HALO + 1:HALO + 1 + tm, :])
        p = _dot((bg * conv).astype(BF16), wout_ref[lo:hi, :])
        if j == 0:
            acc_ref[...] = p
        else:
            acc_ref[...] += p
    o_ref[0] = x + mod[2:3] * _rms(acc_ref[...], ng[1:2])


def _sc_layer(x, mod, ng, w_in, conv_w, w_out):
    bsz, t, d = x.shape
    tm = min(ROW_TILE, t)
    cc = 256
    prev, nxt = _halo_specs(tm, t, d)
    return pl.pallas_call(
        functools.partial(_sc_kernel, cc=cc),
        grid=(bsz, t // tm),
        in_specs=[pl.BlockSpec((1, tm, d), lambda b, i: (b, i, 0)), prev, nxt,
                  pl.BlockSpec((1, 6, d), lambda b, i: (b, 0, 0)),
                  _resident(ng.shape), _resident(w_in.shape), _resident(conv_w.shape),
                  _resident(w_out.shape)],
        out_specs=pl.BlockSpec((1, tm, d), lambda b, i: (b, i, 0)),
        out_shape=jax.ShapeDtypeStruct(x.shape, F32),
        scratch_shapes=[pltpu.VMEM((tm, d), F32), pltpu.VMEM((tm + 2 * HALO, cc), F32)],
        compiler_params=_params("parallel", "parallel"),
        name="short_conv",
    )(x, x, x, mod, ng, w_in, conv_w, w_out)


def _lru_gates(xc, gw_ref, gb_ref, lam_ref, a_ref, b_ref):
    bw = gw_ref.shape[1]
    lam = lam_ref[...]
    z = -lam
    softplus = jnp.maximum(z, 0.0) + jnp.log1p(jnp.exp(-jnp.abs(z)))
    nsp = -LRU_C * softplus
    gb = gb_ref[...]
    for n in range(gw_ref.shape[0]):
        lo, hi = n * bw, (n + 1) * bw
        xh = xc[:, lo:hi]
        pre = _dot(xh.astype(BF16), gw_ref[n])
        r = _sigmoid(pre[:, :bw] + gb[0:1, lo:hi])
        ig = _sigmoid(pre[:, bw:] + gb[1:2, lo:hi])
        log_a = r * nsp[:, lo:hi]
        a_ref[:, lo:hi] = jnp.exp(log_a)
        s = jnp.tanh(-log_a)
        b_ref[:, lo:hi] = jnp.sqrt(2.0 * s / (1.0 + s)) * ig * xh


def _lru_scan(a_ref, b_ref, h_ref, carry_ref, *, reverse):
    tm, r = a_ref.shape
    ngroups = tm // 8
    row = lax.broadcasted_iota(jnp.int32, (8, r), 0)

    def group(k, carry):
        g = (ngroups - 1 - k) if reverse else k
        sl = pl.ds(pl.multiple_of(g * 8, 8), 8)
        a = a_ref[sl, :]
        b = b_ref[sl, :]
        for s in (1, 2, 4):
            if reverse:
                keep = row < 8 - s
                a_sh = pltpu.roll(a, 8 - s, 0)
                b_sh = pltpu.roll(b, 8 - s, 0)
            else:
                keep = row >= s
                a_sh = pltpu.roll(a, s, 0)
                b_sh = pltpu.roll(b, s, 0)
            b = jnp.where(keep, a * b_sh + b, b)
            a = jnp.where(keep, a * a_sh, a)
        h = a * carry + b
        h_ref[sl, :] = h
        edge = h[0:1, :] if reverse else h[7:8, :]
        return jnp.broadcast_to(edge, (8, r))

    carry_ref[...] = lax.fori_loop(0, ngroups, group, carry_ref[...])


def _lru_fwd_kernel(x_ref, xp_ref, xn_ref, mod_ref, ng_ref, win_ref, cw_ref, cb_ref, gw_ref, gb_ref,
                    lam_ref, xc_ref, gy_ref, hf_ref, ext_ref, a_ref, b_ref, carry_ref):
    i = pl.program_id(1)
    nt = pl.num_programs(1)
    tm, r = a_ref.shape
    mod = mod_ref[0]
    ng = ng_ref[...]
    g0, sc, sh = ng[0:1], mod[1:2], mod[0:1]
    h = _modnorm(x_ref[0], g0, sc, sh).astype(BF16)
    he = jnp.concatenate([_modnorm(xp_ref[0], g0, sc, sh), _modnorm(xn_ref[0], g0, sc, sh)],
                         axis=0).astype(BF16)
    gy_ref[0] = _gelu_tanh(_dot(h, win_ref[:, :r])).astype(gy_ref.dtype)
    xb = _dot(h, win_ref[:, r:])
    xe = _dot(he, win_ref[:, r:])
    ext_ref[0:HALO, :] = xe[0:HALO] * (i > 0).astype(F32)
    ext_ref[HALO:HALO + tm, :] = xb
    ext_ref[HALO + tm:, :] = xe[HALO:] * (i < nt - 1).astype(F32)
    cw = cw_ref[...]
    xc = (cw[0:1] * ext_ref[HALO - 2:HALO - 2 + tm, :]
          + cw[1:2] * ext_ref[HALO - 1:HALO - 1 + tm, :]
          + cw[2:3] * xb
          + cw[3:4] * ext_ref[HALO + 1:HALO + 1 + tm, :]) + cb_ref[...]
    xc_ref[0] = xc
    _lru_gates(xc, gw_ref, gb_ref, lam_ref, a_ref, b_ref)

    @pl.when(i == 0)
    def _():
        carry_ref[...] = jnp.zeros_like(carry_ref)

    _lru_scan(a_ref, b_ref, hf_ref.at[0], carry_ref, reverse=False)


def _lru_bwd_kernel(x_ref, xc_ref, gy_ref, hf_ref, mod_ref, ng_ref, gw_ref, gb_ref, lam_ref, wout_ref,
                    o_ref, a_ref, b_ref, hb_ref, carry_ref):
    i = pl.program_id(1)
    mod = mod_ref[0]
    ng = ng_ref[...]
    _lru_gates(xc_ref[0], gw_ref, gb_ref, lam_ref, a_ref, b_ref)

    @pl.when(i == 0)
    def _():
        carry_ref[...] = jnp.zeros_like(carry_ref)

    _lru_scan(a_ref, b_ref, hb_ref, carry_ref, reverse=True)
    rec = hf_ref[0] + hb_ref[...]
    m = _dot((rec * gy_ref[0].astype(F32)).astype(BF16), wout_ref[...])
    o_ref[0] = x_ref[0] + mod[2:3] * _rms(m, ng[1:2])


def _lru_layer(x, mod, ng, w_in, conv_w, conv_b, gate_w, gate_b, lam, w_out):
    bsz, t, d = x.shape
    r = w_out.shape[0]
    tm = min(ROW_TILE, t)
    nt = t // tm
    nh, bw = gate_w.shape[2], gate_w.shape[3]
    gw = jnp.transpose(gate_w, (0, 2, 3, 1, 4)).reshape(2, nh, bw, 2 * bw).astype(BF16)
    prev, nxt = _halo_specs(tm, t, d)
    tile = lambda width: pl.BlockSpec((1, tm, width), lambda b, i: (b, i, 0))
    modspec = pl.BlockSpec((1, 6, d), lambda b, i: (b, 0, 0))
    lam2 = lam.reshape(2, 1, r)
    xc, gy, hf = pl.pallas_call(
        _lru_fwd_kernel,
        grid=(bsz, nt),
        in_specs=[tile(d), prev, nxt, modspec, _resident(ng.shape), _resident(w_in.shape),
                  _resident(conv_w.shape), _resident((1, r)), _resident(gw.shape[1:]),
                  _resident(gate_b.shape[1:]), _resident((1, r))],
        out_specs=[tile(r), tile(r), tile(r)],
        out_shape=[jax.ShapeDtypeStruct((bsz, t, r), F32), jax.ShapeDtypeStruct((bsz, t, r), BF16),
                   jax.ShapeDtypeStruct((bsz, t, r), F32)],
        scratch_shapes=[pltpu.VMEM((tm + 2 * HALO, r), F32), pltpu.VMEM((tm, r), F32),
                        pltpu.VMEM((tm, r), F32), pltpu.VMEM((8, r), F32)],
        compiler_params=_params("parallel", "arbitrary"),
        name="lru_fwd",
    )(x, x, x, mod, ng, w_in, conv_w, conv_b.reshape(1, r), gw[0], gate_b[0], lam2[0])
    rtile = lambda width: pl.BlockSpec((1, tm, width), lambda b, i: (b, nt - 1 - i, 0))
    return pl.pallas_call(
        _lru_bwd_kernel,
        grid=(bsz, nt),
        in_specs=[rtile(d), rtile(r), rtile(r), rtile(r), modspec, _resident(ng.shape),
                  _resident(gw.shape[1:]), _resident(gate_b.shape[1:]), _resident((1, r)),
                  _resident(w_out.shape)],
        out_specs=rtile(d),
        out_shape=jax.ShapeDtypeStruct(x.shape, F32),
        scratch_shapes=[pltpu.VMEM((tm, r), F32), pltpu.VMEM((tm, r), F32), pltpu.VMEM((tm, r), F32),
                        pltpu.VMEM((8, r), F32)],
        compiler_params=_params("parallel", "arbitrary"),
        name="lru_bwd",
    )(x, xc, gy, hf, mod, ng, gw[1], gate_b[1], lam2[1], w_out)


def _qkv_kernel(x_ref, mod_ref, ng_ref, w_ref, q_ref, k_ref, v_ref, *, scale):
    d = x_ref.shape[-1]
    mod = mod_ref[0]
    ng = ng_ref[...]
    h = _modnorm(x_ref[0], ng[0:1], mod[1:2], mod[0:1]).astype(BF16)
    q_ref[0] = (_dot(h, w_ref[:, :d]) * scale).astype(BF16)
    k_ref[0] = _dot(h, w_ref[:, d:2 * d]).astype(BF16)
    v_ref[0] = _dot(h, w_ref[:, 2 * d:]).astype(BF16)


def _na_kernel(q_ref, k_ref, v_ref, bias_ref, x_ref, mod_ref, ng_ref, wo_ref, o_ref, att_ref, *, rows):
    jb = pl.program_id(1)
    kh = min(NA_WIN_ROWS, rows)
    nkeys = kh * GRID_W
    lane_lo = lax.broadcasted_iota(jnp.int32, (GRID_W, 128), 1) < 64

    def row_body(rr, carry):
        r = jb * 8 + rr
        start = jnp.clip(r - kh // 2, 0, rows - kh)
        dy0 = start - r + (NA_WIN_ROWS - 1)
        k0 = pl.multiple_of(start * GRID_W, GRID_W)
        q0 = pl.multiple_of(rr * GRID_W, GRID_W)
        for p in range(NA_HEADS // 2):
            lanes = slice(p * 128, (p + 1) * 128)
            qp = q_ref[0, pl.ds(q0, GRID_W), lanes]
            kp = k_ref[0, pl.ds(k0, nkeys), lanes]
            vp = v_ref[0, pl.ds(k0, nkeys), lanes]
            outs = []
            for hh in range(2):
                keep = lane_lo if hh == 0 else jnp.logical_not(lane_lo)
                qm = jnp.where(keep, qp, jnp.zeros_like(qp))
                s = lax.dot_general(qm, kp, (((1,), (1,)), ((), ())), preferred_element_type=F32)
                bias = jnp.concatenate([bias_ref[2 * p + hh, dy0 + 2 * c] for c in range(kh // 2)], axis=1)
                s = s + bias
                e = jnp.exp(s - jnp.max(s, axis=-1, keepdims=True))
                inv = 1.0 / jnp.sum(e, axis=-1, keepdims=True)
                outs.append(_dot(e.astype(BF16), vp) * inv)
            att_ref[pl.ds(q0, GRID_W), lanes] = jnp.where(lane_lo, outs[0], outs[1]).astype(BF16)
        return carry

    lax.fori_loop(0, 8, row_body, 0)
    mod = mod_ref[0]
    ng = ng_ref[...]
    m = _dot(att_ref[...], wo_ref[...])
    o_ref[0] = x_ref[0] + mod[2:3] * _rms(m, ng[1:2])


def _na_bias_table(rpb):
    col = jnp.arange(GRID_W)
    col_start = jnp.clip(col - NA_WIN_COLS // 2, 0, GRID_W - NA_WIN_COLS)
    col_in = (col[None, :] >= col_start[:, None]) & (col[None, :] < col_start[:, None] + NA_WIN_COLS)
    dx = jnp.clip(col[None, :] - col[:, None], -(NA_WIN_COLS - 1), NA_WIN_COLS - 1) + (NA_WIN_COLS - 1)
    b2 = jnp.where(col_in[None, None], rpb[:, :, dx].astype(F32), NEG_INF)
    return jnp.concatenate([b2[:, :-1], b2[:, 1:]], axis=-1)


def _na_layer(x, mod, ng, w_qkv, rpb, w_o):
    bsz, t, d = x.shape
    rows = t // GRID_W
    assert rows % 8 == 0
    tm = min(ROW_TILE, t)
    tile = pl.BlockSpec((1, tm, d), lambda b, i: (b, i, 0))
    modspec = pl.BlockSpec((1, 6, d), lambda b, i: (b, 0, 0))
    scale = (d // NA_HEADS) ** -0.5
    qkv_shape = jax.ShapeDtypeStruct((bsz, t, d), BF16)
    q, k, v = pl.pallas_call(
        functools.partial(_qkv_kernel, scale=scale),
        grid=(bsz, t // tm),
        in_specs=[tile, modspec, _resident(ng.shape), _resident(w_qkv.shape)],
        out_specs=[tile, tile, tile],
        out_shape=[qkv_shape, qkv_shape, qkv_shape],
        compiler_params=_params("parallel", "parallel"),
        name="na_qkv",
    )(x, mod, ng, w_qkv)
    bias = _na_bias_table(rpb)
    blk = 8 * GRID_W
    qtile = pl.BlockSpec((1, blk, d), lambda b, j: (b, j, 0))
    whole = pl.BlockSpec((1, t, d), lambda b, j: (b, 0, 0), pipeline_mode=pl.Buffered(1))
    return pl.pallas_call(
        functools.partial(_na_kernel, rows=rows),
        grid=(bsz, rows // 8),
        in_specs=[qtile, whole, whole, _resident(bias.shape), qtile, modspec, _resident(ng.shape),
                  _resident(w_o.shape)],
        out_specs=qtile,
        out_shape=jax.ShapeDtypeStruct(x.shape, F32),
        scratch_shapes=[pltpu.VMEM((blk, d), BF16)],
        compiler_params=_params("parallel", "arbitrary"),
        name="na_attn",
    )(q, k, v, bias, x, mod, ng, w_o)


def kernel(x, c, ada_w, ada_b, norm_g, ffn_w_gu, ffn_w_down, lru_w_in, lru_conv_w, lru_conv_b, lru_gate_w,
           lru_gate_b, lru_lambda, lru_w_out, na_w_qkv, na_rpb, na_w_o, sc_w_in, sc_conv_w, sc_w_out):
    depth = ada_w.shape[0]
    bsz, _, d = x.shape
    mods = _ada_mod(c, ada_w, ada_b).reshape(depth, bsz, 6, d)
    bf = lambda w: w.astype(BF16)
    for i in range(depth):
        kind, j = i % 3, i // 3
        mod, ng = mods[i], norm_g[i]
        if kind == 0:
            x = _lru_layer(x, mod, ng, bf(lru_w_in[j]), lru_conv_w[j], lru_conv_b[j], lru_gate_w[j],
                           lru_gate_b[j], lru_lambda[j], bf(lru_w_out[j]))
        elif kind == 1:
            x = _na_layer(x, mod, ng, bf(na_w_qkv[j]), na_rpb[j], bf(na_w_o[j]))
        else:
            x = _sc_layer(x, mod, ng, bf(sc_w_in[j]), sc_conv_w[j], bf(sc_w_out[j]))
        x = _ffn_layer(x, mod, ng, bf(ffn_w_gu[i]), bf(ffn_w_down[i]))
    return x
```

```python
import functools
import math

import jax
import jax.numpy as jnp
from jax import lax
from jax.experimental import pallas as pl
from jax.experimental.pallas import tpu as pltpu

F32 = jnp.float32
BF16 = jnp.bfloat16

RMS_EPS = 1e-6
GRID_W = 64
NA_HEADS = 16
NA_WIN_ROWS = 8
NA_WIN_COLS = 16
NEG_INF = -1e30
LRU_C = 8.0

V7X_VMEM_LIMIT_BYTES = 56 * 1024 * 1024
LANES = 128
ROW_TILE = 512
SOFTMAX_ROWS = 32
LOG2E = math.log2(math.e)


def _params(*sem):
    return pltpu.CompilerParams(dimension_semantics=sem, vmem_limit_bytes=V7X_VMEM_LIMIT_BYTES)


def _resident(shape):
    nd = len(shape)
    return pl.BlockSpec(shape, lambda *_: (0,) * nd, pipeline_mode=pl.Buffered(1))


def _rms(x, g):
    ms = jnp.mean(x * x, axis=-1, keepdims=True)
    return x * lax.rsqrt(ms + RMS_EPS) * g


def _modnorm(x, g, scale, shift):
    return _rms(x, g) * (1.0 + scale) + shift


def _dot(a, b):
    return jnp.dot(a, b, preferred_element_type=F32)


def _sigmoid(x):
    return 0.5 * jnp.tanh(0.5 * x) + 0.5


def _gelu_tanh(x):
    inner = x * (0.7978845608028654 + (0.7978845608028654 * 0.044715) * (x * x))
    hx = 0.5 * x
    return hx + hx * jnp.tanh(inner)


def _ada_kernel(c_ref, w_ref, b_ref, o_ref):
    c = c_ref[...]
    ca = (c * _sigmoid(c)).astype(BF16)
    o_ref[0] = _dot(ca, w_ref[0].astype(BF16)) + b_ref[0]


def _ada_mod(c, ada_w, ada_b):
    depth, d, n = ada_w.shape
    bsz = c.shape[0]
    tn = 2048
    return pl.pallas_call(
        _ada_kernel,
        grid=(depth, n // tn),
        in_specs=[pl.BlockSpec((bsz, d), lambda l, j: (0, 0)),
                  pl.BlockSpec((1, d, tn), lambda l, j: (l, 0, j)),
                  pl.BlockSpec((1, 1, tn), lambda l, j: (l, 0, j))],
        out_specs=pl.BlockSpec((1, bsz, tn), lambda l, j: (l, 0, j)),
        out_shape=jax.ShapeDtypeStruct((depth, bsz, n), F32),
        compiler_params=_params("parallel", "parallel"),
        name="ada_mod",
    )(c, ada_w, ada_b.reshape(depth, 1, n))


def _batch_tile(tm, d):
    return pl.BlockSpec((tm, d), lambda b, i: (i, b))


def _time_tile(tq, bsz, d, index=lambda i: i):
    return pl.BlockSpec((tq, bsz, d), lambda i: (index(i), 0, 0))


_MOD_BATCH = lambda d: pl.BlockSpec((1, 6, d), lambda b, i: (b, 0, 0))


def _ffn_kernel(x_ref, mod_ref, ng_ref, wgu_ref, wd_ref, o_ref, acc_ref, *, hidden, hc):
    x = x_ref[...]
    mod = mod_ref[0]
    ng = ng_ref[...]
    h = _modnorm(x, ng[2:3], mod[4:5], mod[3:4]).astype(BF16)
    for j in range(hidden // hc):
        g = _dot(h, wgu_ref[:, j * hc:(j + 1) * hc])
        u = _dot(h, wgu_ref[:, hidden + j * hc:hidden + (j + 1) * hc])
        a = (g * _sigmoid(g) * u).astype(BF16)
        p = _dot(a, wd_ref[j * hc:(j + 1) * hc, :])
        if j == 0:
            acc_ref[...] = p
        else:
            acc_ref[...] += p
    o_ref[...] = x + mod[5:6] * _rms(acc_ref[...], ng[3:4])


def _ffn_layer(x, mod, ng, w_gu, w_down, batch_major_out):
    t, bsz, d = x.shape
    hidden = w_down.shape[0]
    tm = min(ROW_TILE, t)
    if batch_major_out:
        out_spec = pl.BlockSpec((None, tm, d), lambda b, i: (b, i, 0))
        out_shape = jax.ShapeDtypeStruct((bsz, t, d), F32)
    else:
        out_spec = _batch_tile(tm, d)
        out_shape = jax.ShapeDtypeStruct((t, bsz * d), F32)
    out = pl.pallas_call(
        functools.partial(_ffn_kernel, hidden=hidden, hc=256),
        grid=(bsz, t // tm),
        in_specs=[_batch_tile(tm, d), _MOD_BATCH(d),
                  _resident(ng.shape), _resident(w_gu.shape), _resident(w_down.shape)],
        out_specs=out_spec,
        out_shape=out_shape,
        scratch_shapes=[pltpu.VMEM((tm, d), F32)],
        compiler_params=_params("parallel", "parallel"),
        name="ffn",
    )(x.reshape(t, bsz * d), mod, ng, w_gu, w_down)
    return out if batch_major_out else out.reshape(t, bsz, d)


def _sc_kernel(x_ref, xp_ref, xn_ref, mod_ref, ng_ref, win_ref, cw_ref, wout_ref, o_ref, y_ref, *, cc):
    i = pl.program_id(0)
    nt = pl.num_programs(0)
    tq, bsz, d = x_ref.shape
    rows = tq * bsz
    x = x_ref[...]
    mod = mod_ref[...]
    ng = ng_ref[...]
    g0, sc, sh = ng[0:1], mod[1], mod[0]
    h = _modnorm(x, g0, sc, sh).reshape(rows, d).astype(BF16)
    he = _modnorm(jnp.concatenate([xp_ref[...], xn_ref[...]], axis=0), g0, sc, sh)
    he = he.reshape(2 * bsz, d).astype(BF16)
    keep_prev = (i > 0).astype(F32)
    keep_next = (i < nt - 1).astype(F32)
    cw = cw_ref[...]
    for j in range(d // cc):
        lo, hi = j * cc, (j + 1) * cc
        bg = _dot(h, win_ref[:, lo:hi])
        v = _dot(h, win_ref[:, d + lo:d + hi]) * _dot(h, win_ref[:, 2 * d + lo:2 * d + hi])
        ve = _dot(he, win_ref[:, d + lo:d + hi]) * _dot(he, win_ref[:, 2 * d + lo:2 * d + hi])
        v_prev = jnp.concatenate([ve[:bsz] * keep_prev, v[:rows - bsz]], axis=0)
        v_next = jnp.concatenate([v[bsz:], ve[bsz:] * keep_next], axis=0)
        conv = cw[0:1, lo:hi] * v_prev + cw[1:2, lo:hi] * v + cw[2:3, lo:hi] * v_next
        y_ref[:, lo:hi] = (bg * conv).astype(BF16)
    m = _dot(y_ref[...], wout_ref[...]).reshape(tq, bsz, d)
    o_ref[...] = x + mod[2] * _rms(m, ng[1:2])


def _sc_layer(x, modt, ng, w_in, conv_w, w_out):
    t, bsz, d = x.shape
    tq = ROW_TILE // bsz
    halo = lambda index: pl.BlockSpec((1, bsz, d), lambda i: (index(i), 0, 0))
    return pl.pallas_call(
        functools.partial(_sc_kernel, cc=256),
        grid=(t // tq,),
        in_specs=[_time_tile(tq, bsz, d),
                  halo(lambda i: jnp.maximum(i * tq - 1, 0)),
                  halo(lambda i: jnp.minimum((i + 1) * tq, t - 1)),
                  _resident(modt.shape), _resident(ng.shape), _resident(w_in.shape),
                  _resident(conv_w.shape), _resident(w_out.shape)],
        out_specs=_time_tile(tq, bsz, d),
        out_shape=jax.ShapeDtypeStruct(x.shape, F32),
        scratch_shapes=[pltpu.VMEM((tq * bsz, d), BF16)],
        compiler_params=_params("parallel"),
        name="short_conv",
    )(x, x, x, modt, ng, w_in, conv_w, w_out)


def _lru_gates(xc, gw_ref, gb_ref, lam_ref, a_ref, b_ref):
    tq, bsz, _ = a_ref.shape
    bw = gw_ref.shape[1]
    z = -lam_ref[...]
    softplus = jnp.maximum(z, 0.0) + jnp.log1p(jnp.exp(-jnp.abs(z)))
    neg_log_a_scale = LRU_C * softplus
    gb = gb_ref[...]
    for n in range(gw_ref.shape[0]):
        lo, hi = n * bw, (n + 1) * bw
        xh = xc[:, lo:hi]
        pre = _dot(xh.astype(BF16), gw_ref[n])
        r = _sigmoid(pre[:, :bw] + gb[0:1, lo:hi])
        ig = _sigmoid(pre[:, bw:] + gb[1:2, lo:hi])
        a = jnp.exp2(r * (-LOG2E * neg_log_a_scale[:, lo:hi]))
        s = jnp.tanh(r * neg_log_a_scale[:, lo:hi])
        w = s + s
        gain = w * lax.rsqrt(jnp.maximum(w * (1.0 + s), 1e-30))
        a_ref[:, :, lo:hi] = a.reshape(tq, bsz, bw)
        b_ref[:, :, lo:hi] = (gain * ig * xh).reshape(tq, bsz, bw)


def _lru_scan(a_ref, b_ref, h_ref, carry_ref, *, reverse):
    tq = a_ref.shape[0]

    def step(k, h):
        t = (tq - 1 - k) if reverse else k
        h = a_ref[t] * h + b_ref[t]
        h_ref[t] = h
        return h

    carry_ref[...] = lax.fori_loop(0, tq, step, carry_ref[...], unroll=8)


def _lru_fwd_kernel(x_ref, xp_ref, xn_ref, mod_ref, ng_ref, win_ref, cw_ref, cb_ref, gw_ref, gb_ref,
                    lam_ref, xc_ref, gy_ref, hf_ref, a_ref, b_ref, carry_ref):
    i = pl.program_id(0)
    nt = pl.num_programs(0)
    tq, bsz, r = a_ref.shape
    d = x_ref.shape[-1]
    rows = tq * bsz
    mod = mod_ref[...]
    ng = ng_ref[...]
    g0, sc, sh = ng[0:1], mod[1], mod[0]
    h = _modnorm(x_ref[...], g0, sc, sh).reshape(rows, d).astype(BF16)
    he = _modnorm(jnp.concatenate([xp_ref[...], xn_ref[...]], axis=0), g0, sc, sh)
    he = he.reshape(4 * bsz, d).astype(BF16)
    gy_ref[...] = _gelu_tanh(_dot(h, win_ref[:, :r])).astype(gy_ref.dtype).reshape(tq, bsz, r)
    xb = _dot(h, win_ref[:, r:])
    xe = _dot(he, win_ref[:, r:])
    x_prev = xe[:2 * bsz] * (i > 0).astype(F32)
    x_next = xe[2 * bsz:3 * bsz] * (i < nt - 1).astype(F32)
    cw = cw_ref[...]
    xc = (cw[0:1] * jnp.concatenate([x_prev, xb[:rows - 2 * bsz]], axis=0)
          + cw[1:2] * jnp.concatenate([x_prev[bsz:], xb[:rows - bsz]], axis=0)
          + cw[2:3] * xb
          + cw[3:4] * jnp.concatenate([xb[bsz:], x_next], axis=0)) + cb_ref[...]
    xc_ref[...] = xc.reshape(tq, bsz, r)
    _lru_gates(xc, gw_ref, gb_ref, lam_ref, a_ref, b_ref)

    @pl.when(i == 0)
    def _():
        carry_ref[...] = jnp.zeros_like(carry_ref)

    _lru_scan(a_ref, b_ref, hf_ref, carry_ref, reverse=False)


def _lru_bwd_kernel(x_ref, xc_ref, gy_ref, hf_ref, mod_ref, ng_ref, gw_ref, gb_ref, lam_ref, wout_ref,
                    o_ref, a_ref, b_ref, hb_ref, carry_ref):
    i = pl.program_id(0)
    tq, bsz, r = a_ref.shape
    d = x_ref.shape[-1]
    rows = tq * bsz
    mod = mod_ref[...]
    ng = ng_ref[...]
    _lru_gates(xc_ref[...].reshape(rows, r), gw_ref, gb_ref, lam_ref, a_ref, b_ref)

    @pl.when(i == 0)
    def _():
        carry_ref[...] = jnp.zeros_like(carry_ref)

    _lru_scan(a_ref, b_ref, hb_ref, carry_ref, reverse=True)
    rec = hf_ref[...] + hb_ref[...]
    z = (rec * gy_ref[...].astype(F32)).reshape(rows, r).astype(BF16)
    m = _dot(z, wout_ref[...]).reshape(tq, bsz, d)
    o_ref[...] = x_ref[...] + mod[2] * _rms(m, ng[1:2])


def _lru_layer(x, modt, ng, w_in, conv_w, conv_b, gate_w, gate_b, lam, w_out):
    t, bsz, d = x.shape
    r = w_out.shape[0]
    tq = ROW_TILE // bsz
    nt = t // tq
    nh, bw = gate_w.shape[2], gate_w.shape[3]
    gw = jnp.transpose(gate_w, (0, 2, 3, 1, 4)).reshape(2, nh, bw, 2 * bw).astype(BF16)
    lam2 = lam.reshape(2, 1, r)
    halo = lambda index: pl.BlockSpec((2, bsz, d), lambda i: (index(i), 0, 0))
    scratch = [pltpu.VMEM((tq, bsz, r), F32), pltpu.VMEM((tq, bsz, r), F32)]
    carry = pltpu.VMEM((bsz, r), F32)
    xc, gy, hf = pl.pallas_call(
        _lru_fwd_kernel,
        grid=(nt,),
        in_specs=[_time_tile(tq, bsz, d),
                  halo(lambda i: jnp.maximum(i * (tq // 2) - 1, 0)),
                  halo(lambda i: jnp.minimum((i + 1) * (tq // 2), t // 2 - 1)),
                  _resident(modt.shape), _resident(ng.shape), _resident(w_in.shape),
                  _resident(conv_w.shape), _resident((1, r)), _resident(gw.shape[1:]),
                  _resident(gate_b.shape[1:]), _resident((1, r))],
        out_specs=[_time_tile(tq, bsz, r)] * 3,
        out_shape=[jax.ShapeDtypeStruct((t, bsz, r), F32), jax.ShapeDtypeStruct((t, bsz, r), BF16),
                   jax.ShapeDtypeStruct((t, bsz, r), F32)],
        scratch_shapes=scratch + [carry],
        compiler_params=_params("arbitrary"),
        name="lru_fwd",
    )(x, x, x, modt, ng, w_in, conv_w, conv_b.reshape(1, r), gw[0], gate_b[0], lam2[0])
    rev = lambda i: nt - 1 - i
    return pl.pallas_call(
        _lru_bwd_kernel,
        grid=(nt,),
        in_specs=[_time_tile(tq, bsz, d, rev), _time_tile(tq, bsz, r, rev), _time_tile(tq, bsz, r, rev),
                  _time_tile(tq, bsz, r, rev), _resident(modt.shape), _resident(ng.shape),
                  _resident(gw.shape[1:]), _resident(gate_b.shape[1:]), _resident((1, r)),
                  _resident(w_out.shape)],
        out_specs=_time_tile(tq, bsz, d, rev),
        out_shape=jax.ShapeDtypeStruct(x.shape, F32),
        scratch_shapes=scratch + [pltpu.VMEM((tq, bsz, r), F32), carry],
        compiler_params=_params("arbitrary"),
        name="lru_bwd",
    )(x, xc, gy, hf, modt, ng, gw[1], gate_b[1], lam2[1], w_out)


def _qkv_kernel(x_ref, mod_ref, ng_ref, w_ref, q_ref, k_ref, v_ref, *, scale):
    d = x_ref.shape[-1]
    mod = mod_ref[0]
    ng = ng_ref[...]
    h = _modnorm(x_ref[...], ng[0:1], mod[1:2], mod[0:1]).astype(BF16)
    q_ref[...] = (_dot(h, w_ref[:, :d]) * scale).astype(BF16)
    k_ref[...] = _dot(h, w_ref[:, d:2 * d]).astype(BF16)
    v_ref[...] = _dot(h, w_ref[:, 2 * d:]).astype(BF16)


def _na_kernel(q_ref, k_ref, v_ref, bias_ref, x_ref, mod_ref, ng_ref, wo_ref, o_ref,
               att_ref, s_ref, p_ref, *, rows):
    jb = pl.program_id(1)
    kh = min(NA_WIN_ROWS, rows)
    nkeys = kh * GRID_W
    npairs = NA_HEADS // 2
    pr = 2 * GRID_W
    lane_lo = lax.broadcasted_iota(jnp.int32, (GRID_W, LANES), 1) < LANES // 2

    def row_body(rr, carry):
        r = jb * 8 + rr
        start = jnp.clip(r - kh // 2, 0, rows - kh)
        dy0 = start - r + (NA_WIN_ROWS - 1)
        k0 = pl.multiple_of(start * GRID_W, GRID_W)
        q0 = pl.multiple_of(rr * GRID_W, GRID_W)
        for p in range(npairs):
            lanes = slice(p * LANES, (p + 1) * LANES)
            qp = q_ref[pl.ds(q0, GRID_W), lanes]
            zero = jnp.zeros_like(qp)
            qbd = jnp.concatenate([jnp.where(lane_lo, qp, zero), jnp.where(lane_lo, zero, qp)], axis=0)
            s = lax.dot_general(qbd, k_ref[pl.ds(k0, nkeys), lanes], (((1,), (1,)), ((), ())),
                                preferred_element_type=F32)
            bias = jnp.concatenate([bias_ref[p, dy0 + 2 * c] for c in range(kh // 2)], axis=1)
            s_ref[p * pr:(p + 1) * pr, :] = s + bias
        for c in range(npairs * pr // SOFTMAX_ROWS):
            rs = slice(c * SOFTMAX_ROWS, (c + 1) * SOFTMAX_ROWS)
            s = s_ref[rs, :]
            p_ref[rs, :] = jnp.exp(s - jnp.max(s, axis=-1, keepdims=True)).astype(BF16)
        ones = jnp.ones((nkeys, LANES), BF16)
        for p in range(npairs):
            lanes = slice(p * LANES, (p + 1) * LANES)
            v_ones = jnp.concatenate([v_ref[pl.ds(k0, nkeys), lanes], ones], axis=1)
            ol = _dot(p_ref[p * pr:(p + 1) * pr, :], v_ones)
            o = ol[:, :LANES] * (1.0 / ol[:, LANES:])
            att_ref[pl.ds(q0, GRID_W), lanes] = jnp.where(lane_lo, o[:GRID_W], o[GRID_W:]).astype(BF16)
        return carry

    lax.fori_loop(0, 8, row_body, 0)
    mod = mod_ref[0]
    ng = ng_ref[...]
    m = _dot(att_ref[...], wo_ref[...])
    o_ref[...] = x_ref[...] + mod[2:3] * _rms(m, ng[1:2])


def _na_bias_table(rpb):
    col = jnp.arange(GRID_W)
    col_start = jnp.clip(col - NA_WIN_COLS // 2, 0, GRID_W - NA_WIN_COLS)
    col_in = (col[None, :] >= col_start[:, None]) & (col[None, :] < col_start[:, None] + NA_WIN_COLS)
    dx = jnp.clip(col[None, :] - col[:, None], -(NA_WIN_COLS - 1), NA_WIN_COLS - 1) + (NA_WIN_COLS - 1)
    b2 = jnp.where(col_in[None, None], rpb[:, :, dx].astype(F32), NEG_INF)
    b2 = jnp.concatenate([b2[:, :-1], b2[:, 1:]], axis=-1)
    nh, ndy = b2.shape[:2]
    b2 = b2.reshape(nh // 2, 2, ndy, GRID_W, 2 * GRID_W)
    return jnp.transpose(b2, (0, 2, 1, 3, 4)).reshape(nh // 2, ndy, 2 * GRID_W, 2 * GRID_W)


def _na_layer(x, mod, ng, w_qkv, rpb, w_o):
    t, bsz, d = x.shape
    rows = t // GRID_W
    assert rows % 8 == 0
    tm = min(ROW_TILE, t)
    x2 = x.reshape(t, bsz * d)
    scale = (d // NA_HEADS) ** -0.5
    qkv_shape = jax.ShapeDtypeStruct((t, bsz * d), BF16)
    q, k, v = pl.pallas_call(
        functools.partial(_qkv_kernel, scale=scale),
        grid=(bsz, t // tm),
        in_specs=[_batch_tile(tm, d), _MOD_BATCH(d), _resident(ng.shape), _resident(w_qkv.shape)],
        out_specs=[_batch_tile(tm, d)] * 3,
        out_shape=[qkv_shape] * 3,
        compiler_params=_params("parallel", "parallel"),
        name="na_qkv",
    )(x2, mod, ng, w_qkv)
    bias = _na_bias_table(rpb)
    blk = 8 * GRID_W
    nkeys = min(NA_WIN_ROWS, rows) * GRID_W
    whole = pl.BlockSpec((t, d), lambda b, j: (0, b), pipeline_mode=pl.Buffered(1))
    out = pl.pallas_call(
        functools.partial(_na_kernel, rows=rows),
        grid=(bsz, rows // 8),
        in_specs=[_batch_tile(blk, d), whole, whole, _resident(bias.shape), _batch_tile(blk, d),
                  _MOD_BATCH(d), _resident(ng.shape), _resident(w_o.shape)],
        out_specs=_batch_tile(blk, d),
        out_shape=jax.ShapeDtypeStruct((t, bsz * d), F32),
        scratch_shapes=[pltpu.VMEM((blk, d), BF16),
                        pltpu.VMEM((NA_HEADS * GRID_W, nkeys), F32),
                        pltpu.VMEM((NA_HEADS * GRID_W, nkeys), BF16)],
        compiler_params=_params("parallel", "arbitrary"),
        name="na_attn",
    )(q, k, v, bias, x2, mod, ng, w_o)
    return out.reshape(t, bsz, d)


def kernel(x, c, ada_w, ada_b, norm_g, ffn_w_gu, ffn_w_down, lru_w_in, lru_conv_w, lru_conv_b, lru_gate_w,
           lru_gate_b, lru_lambda, lru_w_out, na_w_qkv, na_rpb, na_w_o, sc_w_in, sc_conv_w, sc_w_out):
    depth = ada_w.shape[0]
    bsz, _, d = x.shape
    mods = _ada_mod(c, ada_w, ada_b).reshape(depth, bsz, 6, d)
    x = jnp.transpose(x, (1, 0, 2))
    bf = lambda w: w.astype(BF16)
    for i in range(depth):
        kind, j = i % 3, i // 3
        mod, ng = mods[i], norm_g[i]
        modt = jnp.transpose(mod, (1, 0, 2))
        if kind == 0:
            x = _lru_layer(x, modt, ng, bf(lru_w_in[j]), lru_conv_w[j], lru_conv_b[j], lru_gate_w[j],
                           lru_gate_b[j], lru_lambda[j], bf(lru_w_out[j]))
        elif kind == 1:
            x = _na_layer(x, mod, ng, bf(na_w_qkv[j]), na_rpb[j], bf(na_w_o[j]))
        else:
            x = _sc_layer(x, modt, ng, bf(sc_w_in[j]), sc_conv_w[j], bf(sc_w_out[j]))
        x = _ffn_layer(x, mod, ng, bf(ffn_w_gu[i]), bf(ffn_w_down[i]), batch_major_out=(i == depth - 1))
    return x
```

```python
import functools
import math

import jax
import jax.numpy as jnp
from jax import lax
from jax.experimental import pallas as pl
from jax.experimental.pallas import tpu as pltpu

F32 = jnp.float32
BF16 = jnp.bfloat16

RMS_EPS = 1e-6
GRID_W = 64
NA_HEADS = 16
NA_WIN_ROWS = 8
NA_WIN_COLS = 16
NEG_INF = -1e30
LRU_C = 8.0

V7X_VMEM_LIMIT_BYTES = 56 * 1024 * 1024
LANES = 128
ROW_TILE = 512
LOG2E = math.log2(math.e)


def _params(*sem):
    return pltpu.CompilerParams(dimension_semantics=sem, vmem_limit_bytes=V7X_VMEM_LIMIT_BYTES)


def _resident(shape, layer=None):
    if layer is None:
        nd = len(shape)
        return pl.BlockSpec(shape, lambda *_: (0,) * nd, pipeline_mode=pl.Buffered(1))
    nd = len(shape) - 1
    return pl.BlockSpec((None,) + tuple(shape[1:]), lambda *_: (layer,) + (0,) * nd,
                        pipeline_mode=pl.Buffered(1))


def _rms(x, g):
    ms = jnp.mean(x * x, axis=-1, keepdims=True)
    return x * lax.rsqrt(ms + RMS_EPS) * g


def _modnorm(x, g, scale, shift):
    return _rms(x, g) * (1.0 + scale) + shift


def _dot(a, b):
    return jnp.dot(a, b, preferred_element_type=F32)


def _sigmoid(x):
    return 0.5 * jnp.tanh(0.5 * x) + 0.5


def _gelu_tanh(x):
    inner = x * (0.7978845608028654 + (0.7978845608028654 * 0.044715) * (x * x))
    hx = 0.5 * x
    return hx + hx * jnp.tanh(inner)


def _mod_rows(mod_ref, modt_ref, k, time_major):
    return modt_ref[k][None] if time_major else mod_ref[:, k:k + 1, :]


def _store_swapped(y, n0, n1, swap_ref, store):
    d = y.shape[-1]
    for s in range(d // LANES):
        lanes = slice(s * LANES, (s + 1) * LANES)
        if n1 < n0:
            swap_ref[s] = y[:, lanes]
            for b in range(n1):
                store(b, lanes, swap_ref[s, pl.ds(b, n0, stride=n1), :])
        else:
            for b in range(n0):
                swap_ref[s, pl.ds(b, n1, stride=n0), :] = y[b * n1:(b + 1) * n1, lanes]
            store(None, lanes, swap_ref[s])


def _ada_kernel(c_ref, w_ref, b_ref, o_ref):
    c = c_ref[...]
    ca = (c * _sigmoid(c)).astype(BF16)
    o_ref[0] = _dot(ca, w_ref[0].astype(BF16)) + b_ref[0]


def _ada_mod(c, ada_w, ada_b):
    depth, d, n = ada_w.shape
    bsz = c.shape[0]
    tn = 2048
    return pl.pallas_call(
        _ada_kernel,
        grid=(depth, n // tn),
        in_specs=[pl.BlockSpec((bsz, d), lambda l, j: (0, 0)),
                  pl.BlockSpec((1, d, tn), lambda l, j: (l, 0, j)),
                  pl.BlockSpec((1, 1, tn), lambda l, j: (l, 0, j))],
        out_specs=pl.BlockSpec((1, bsz, tn), lambda l, j: (l, 0, j)),
        out_shape=jax.ShapeDtypeStruct((depth, bsz, n), F32),
        compiler_params=_params("parallel", "parallel"),
        name="ada_mod",
    )(c, ada_w, ada_b.reshape(depth, 1, n))


def _tile(time_major, tq, bsz, d, index=lambda i: i):
    if time_major:
        return pl.BlockSpec((tq, bsz, d), lambda i: (index(i), 0, 0))
    return pl.BlockSpec((bsz, tq, d), lambda i: (0, index(i), 0))


def _stream_shape(time_major, t, bsz, d, dtype=F32):
    return jax.ShapeDtypeStruct((t, bsz, d) if time_major else (bsz, t, d), dtype)


def _ffn_kernel(x_ref, mod_ref, modt_ref, ng_ref, wgu_ref, wd_ref, o_ref, acc_ref, *swap_ref,
                hidden, hc, time_major):
    n0, n1, d = x_ref.shape
    x = x_ref[...]
    ng = ng_ref[...]
    shift, scale, gate = (_mod_rows(mod_ref, modt_ref, k, time_major) for k in (3, 4, 5))
    h = _modnorm(x, ng[2:3], scale, shift).reshape(n0 * n1, d).astype(BF16)
    for j in range(hidden // hc):
        g = _dot(h, wgu_ref[:, j * hc:(j + 1) * hc])
        u = _dot(h, wgu_ref[:, hidden + j * hc:hidden + (j + 1) * hc])
        a = (g * _sigmoid(g) * u).astype(BF16)
        p = _dot(a, wd_ref[j * hc:(j + 1) * hc, :])
        if j == 0:
            acc_ref[...] = p
        else:
            acc_ref[...] += p
    y = x + gate * _rms(acc_ref[...].reshape(n0, n1, d), ng[3:4])
    if not swap_ref:
        o_ref[...] = y
    elif time_major:
        def store(b, lanes, rows):
            o_ref[b, :, lanes] = rows
        _store_swapped(y.reshape(n0 * n1, d), n0, n1, swap_ref[0], store)
    else:
        def store(_, lanes, rows):
            o_ref[:, :, lanes] = rows.reshape(n1, n0, LANES)
        _store_swapped(y.reshape(n0 * n1, d), n0, n1, swap_ref[0], store)


def _ffn_layer(x, mod, modt, ng, w_gu, w_down, layer, time_major_in, time_major_out):
    bsz = mod.shape[0]
    d = x.shape[-1]
    t = x.shape[0] if time_major_in else x.shape[1]
    hidden = w_down.shape[1]
    tq = ROW_TILE // bsz
    scratch = [pltpu.VMEM((ROW_TILE, d), F32)]
    if time_major_in != time_major_out:
        scratch.append(pltpu.VMEM((d // LANES, ROW_TILE, LANES), F32))
    return pl.pallas_call(
        functools.partial(_ffn_kernel, hidden=hidden, hc=256, time_major=time_major_in),
        grid=(t // tq,),
        in_specs=[_tile(time_major_in, tq, bsz, d), _resident(mod.shape), _resident(modt.shape),
                  _resident(ng.shape), _resident(w_gu.shape, layer), _resident(w_down.shape, layer)],
        out_specs=_tile(time_major_out, tq, bsz, d),
        out_shape=_stream_shape(time_major_out, t, bsz, d),
        scratch_shapes=scratch,
        compiler_params=_params("parallel"),
        name="ffn",
    )(x, mod, modt, ng, w_gu, w_down)


def _sc_kernel(x_ref, xp_ref, xn_ref, modt_ref, ng_ref, win_ref, cw_ref, wout_ref, o_ref, y_ref, *, cc):
    i = pl.program_id(0)
    nt = pl.num_programs(0)
    tq, bsz, d = x_ref.shape
    rows = tq * bsz
    x = x_ref[...]
    ng = ng_ref[...]
    g0, sc, sh = ng[0:1], modt_ref[1][None], modt_ref[0][None]
    h = _modnorm(x, g0, sc, sh).reshape(rows, d).astype(BF16)
    he = _modnorm(jnp.concatenate([xp_ref[...], xn_ref[...]], axis=0), g0, sc, sh)
    he = he.reshape(2 * bsz, d).astype(BF16)
    keep_prev = (i > 0).astype(F32)
    keep_next = (i < nt - 1).astype(F32)
    cw = cw_ref[...]
    for j in range(d // cc):
        lo, hi = j * cc, (j + 1) * cc
        bg = _dot(h, win_ref[:, lo:hi])
        v = _dot(h, win_ref[:, d + lo:d + hi]) * _dot(h, win_ref[:, 2 * d + lo:2 * d + hi])
        ve = _dot(he, win_ref[:, d + lo:d + hi]) * _dot(he, win_ref[:, 2 * d + lo:2 * d + hi])
        v_prev = jnp.concatenate([ve[:bsz] * keep_prev, v[:rows - bsz]], axis=0)
        v_next = jnp.concatenate([v[bsz:], ve[bsz:] * keep_next], axis=0)
        conv = cw[0:1, lo:hi] * v_prev + cw[1:2, lo:hi] * v + cw[2:3, lo:hi] * v_next
        y_ref[:, lo:hi] = (bg * conv).astype(BF16)
    m = _dot(y_ref[...], wout_ref[...]).reshape(tq, bsz, d)
    o_ref[...] = x + modt_ref[2][None] * _rms(m, ng[1:2])


def _sc_layer(x, modt, ng, w_in, conv_w, w_out, j):
    t, bsz, d = x.shape
    tq = ROW_TILE // bsz
    halo = lambda index: pl.BlockSpec((1, bsz, d), lambda i: (index(i), 0, 0))
    return pl.pallas_call(
        functools.partial(_sc_kernel, cc=256),
        grid=(t // tq,),
        in_specs=[_tile(True, tq, bsz, d),
                  halo(lambda i: jnp.maximum(i * tq - 1, 0)),
                  halo(lambda i: jnp.minimum((i + 1) * tq, t - 1)),
                  _resident(modt.shape), _resident(ng.shape), _resident(w_in.shape, j),
                  _resident(conv_w.shape, j), _resident(w_out.shape, j)],
        out_specs=_tile(True, tq, bsz, d),
        out_shape=jax.ShapeDtypeStruct(x.shape, F32),
        scratch_shapes=[pltpu.VMEM((tq * bsz, d), BF16)],
        compiler_params=_params("parallel"),
        name="short_conv",
    )(x, x, x, modt, ng, w_in, conv_w, w_out)


def _lru_gates(xc, gw_ref, gb_ref, lam_ref, a_ref, b_ref):
    tq, bsz, _ = a_ref.shape
    bw = gw_ref.shape[1]
    z = -lam_ref[...]
    softplus = jnp.maximum(z, 0.0) + jnp.log1p(jnp.exp(-jnp.abs(z)))
    half_scale = (0.5 * LRU_C) * softplus
    half_scale_log2 = -LOG2E * half_scale
    gb = gb_ref[...]
    for n in range(gw_ref.shape[0]):
        lo, hi = n * bw, (n + 1) * bw
        xh = xc[:, lo:hi]
        pre = _dot(xh.astype(BF16), gw_ref[n])
        tr = jnp.tanh(pre[:, :bw] + gb[0:1, lo:hi])
        ti = jnp.tanh(pre[:, bw:] + gb[1:2, lo:hi])
        a = jnp.exp2(tr * half_scale_log2[:, lo:hi] + half_scale_log2[:, lo:hi])
        s = jnp.tanh(tr * half_scale[:, lo:hi] + half_scale[:, lo:hi])
        half_gain = s * lax.rsqrt(jnp.maximum((s + s) * (1.0 + s), 1e-30))
        a_ref[:, :, lo:hi] = a.reshape(tq, bsz, bw)
        b_ref[:, :, lo:hi] = ((ti + 1.0) * (half_gain * xh)).reshape(tq, bsz, bw)


def _lru_scan(a_ref, b_ref, h_ref, carry_ref, *, reverse):
    tq = a_ref.shape[0]

    def step(k, h):
        t = (tq - 1 - k) if reverse else k
        h = a_ref[t] * h + b_ref[t]
        h_ref[t] = h
        return h

    carry_ref[...] = lax.fori_loop(0, tq, step, carry_ref[...], unroll=8)


def _lru_fwd_kernel(x_ref, xp_ref, xn_ref, modt_ref, ng_ref, win_ref, cw_ref, cb_ref, gw_ref, gb_ref,
                    lam_ref, xc_ref, gy_ref, hf_ref, a_ref, b_ref, carry_ref, *swap_ref):
    i = pl.program_id(0)
    nt = pl.num_programs(0)
    tq, bsz, r = a_ref.shape
    d = x_ref.shape[-1]
    rows = tq * bsz
    ng = ng_ref[...]
    g0, sc, sh = ng[0:1], modt_ref[1][None], modt_ref[0][None]
    if swap_ref:
        for s in range(d // LANES):
            for b in range(bsz):
                swap_ref[0][s, pl.ds(b, tq, stride=bsz), :] = x_ref[b, :, s * LANES:(s + 1) * LANES]
        x = jnp.concatenate([swap_ref[0][s] for s in range(d // LANES)], axis=1).reshape(tq, bsz, d)
    else:
        x = x_ref[...]
    h = _modnorm(x, g0, sc, sh).reshape(rows, d).astype(BF16)
    he = _modnorm(jnp.concatenate([xp_ref[...], xn_ref[...]], axis=0), g0, sc, sh)
    he = he.reshape(4 * bsz, d).astype(BF16)
    gy_ref[...] = _gelu_tanh(_dot(h, win_ref[:, :r])).astype(gy_ref.dtype).reshape(tq, bsz, r)
    xb = _dot(h, win_ref[:, r:])
    xe = _dot(he, win_ref[:, r:])
    x_prev = xe[:2 * bsz] * (i > 0).astype(F32)
    x_next = xe[2 * bsz:3 * bsz] * (i < nt - 1).astype(F32)
    cw = cw_ref[...]
    xc = (cw[0:1] * jnp.concatenate([x_prev, xb[:rows - 2 * bsz]], axis=0)
          + cw[1:2] * jnp.concatenate([x_prev[bsz:], xb[:rows - bsz]], axis=0)
          + cw[2:3] * xb
          + cw[3:4] * jnp.concatenate([xb[bsz:], x_next], axis=0)) + cb_ref[...]
    xc_ref[...] = xc.reshape(tq, bsz, r)
    _lru_gates(xc, gw_ref, gb_ref, lam_ref, a_ref, b_ref)

    @pl.when(i == 0)
    def _():
        carry_ref[...] = jnp.zeros_like(carry_ref)

    _lru_scan(a_ref, b_ref, hf_ref, carry_ref, reverse=False)


def _lru_bwd_kernel(x_ref, xc_ref, gy_ref, hf_ref, mod_ref, modt_ref, ng_ref, gw_ref, gb_ref, lam_ref,
                    wout_ref, o_ref, a_ref, b_ref, hb_ref, carry_ref, *swap_ref):
    i = pl.program_id(0)
    tq, bsz, r = a_ref.shape
    d = x_ref.shape[-1]
    rows = tq * bsz
    ng = ng_ref[...]
    _lru_gates(xc_ref[...].reshape(rows, r), gw_ref, gb_ref, lam_ref, a_ref, b_ref)

    @pl.when(i == 0)
    def _():
        carry_ref[...] = jnp.zeros_like(carry_ref)

    _lru_scan(a_ref, b_ref, hb_ref, carry_ref, reverse=True)
    rec = hf_ref[...] + hb_ref[...]
    z = (rec * gy_ref[...].astype(F32)).reshape(rows, r).astype(BF16)
    m = _dot(z, wout_ref[...])
    if not swap_ref:
        o_ref[...] = x_ref[...] + modt_ref[2][None] * _rms(m.reshape(tq, bsz, d), ng[1:2])
    else:
        m = _rms(m, ng[1:2])

        def store(b, lanes, m_rows):
            o_ref[b, :, lanes] = x_ref[b, :, lanes] + mod_ref[b, 2:3, lanes] * m_rows
        _store_swapped(m, tq, bsz, swap_ref[0], store)


def _lru_layer(x, mod, modt, ng, w_in, conv_w, conv_b, gate_w, gate_b, lam, w_out, j, time_major):
    d = x.shape[-1]
    bsz = mod.shape[0]
    t = x.shape[0] if time_major else x.shape[1]
    r = w_out.shape[1]
    tq = ROW_TILE // bsz
    nt = t // tq
    nh, bw = gate_w.shape[3], gate_w.shape[4]
    gw = (0.5 * jnp.transpose(gate_w[j], (0, 2, 3, 1, 4))).reshape(2, nh, bw, 2 * bw).astype(BF16)
    gb = 0.5 * gate_b[j]
    lam2 = lam[j].reshape(2, 1, r)
    if time_major:
        halo = lambda index: pl.BlockSpec((2, bsz, d), lambda i: (index(i), 0, 0))
        halos = [halo(lambda i: jnp.maximum(i * (tq // 2) - 1, 0)),
                 halo(lambda i: jnp.minimum((i + 1) * (tq // 2), t // 2 - 1))]
        halo_args = [x, x]
    else:
        xr = x.reshape(bsz, nt, tq, d)
        halo = lambda index: pl.BlockSpec((None, 2, bsz, d), lambda i: (index(i), 0, 0, 0))
        halos = [halo(lambda i: jnp.maximum(i - 1, 0)), halo(lambda i: jnp.minimum(i + 1, nt - 1))]
        halo_args = [jnp.transpose(xr[:, :, tq - 2:], (1, 2, 0, 3)), jnp.transpose(xr[:, :, :2], (1, 2, 0, 3))]
    scratch = [pltpu.VMEM((tq, bsz, r), F32), pltpu.VMEM((tq, bsz, r), F32)]
    carry = pltpu.VMEM((bsz, r), F32)
    swap = [] if time_major else [pltpu.VMEM((d // LANES, ROW_TILE, LANES), F32)]
    xc, gy, hf = pl.pallas_call(
        _lru_fwd_kernel,
        grid=(nt,),
        in_specs=[_tile(time_major, tq, bsz, d)] + halos
        + [_resident(modt.shape), _resident(ng.shape), _resident(w_in.shape, j),
           _resident(conv_w.shape, j), _resident((1, r)), _resident(gw.shape[1:]),
           _resident(gb.shape[1:]), _resident((1, r))],
        out_specs=[_tile(True, tq, bsz, r)] * 3,
        out_shape=[jax.ShapeDtypeStruct((t, bsz, r), F32), jax.ShapeDtypeStruct((t, bsz, r), BF16),
                   jax.ShapeDtypeStruct((t, bsz, r), F32)],
        scratch_shapes=scratch + [carry] + swap,
        compiler_params=_params("arbitrary"),
        name="lru_fwd",
    )(x, *halo_args, modt, ng, w_in, conv_w, conv_b[j].reshape(1, r), gw[0], gb[0], lam2[0])
    rev = lambda i: nt - 1 - i
    return pl.pallas_call(
        _lru_bwd_kernel,
        grid=(nt,),
        in_specs=[_tile(time_major, tq, bsz, d, rev), _tile(True, tq, bsz, r, rev),
                  _tile(True, tq, bsz, r, rev), _tile(True, tq, bsz, r, rev),
                  _resident(mod.shape), _resident(modt.shape), _resident(ng.shape),
                  _resident(gw.shape[1:]), _resident(gb.shape[1:]), _resident((1, r)),
                  _resident(w_out.shape, j)],
        out_specs=_tile(time_major, tq, bsz, d, rev),
        out_shape=_stream_shape(time_major, t, bsz, d),
        scratch_shapes=scratch + [pltpu.VMEM((tq, bsz, r), F32), carry] + swap,
        compiler_params=_params("arbitrary"),
        name="lru_bwd",
    )(x, xc, gy, hf, mod, modt, ng, gw[1], gb[1], lam2[1], w_out)


def _qkv_kernel(x_ref, mod_ref, ng_ref, w_ref, q_ref, k_ref, v_ref, *, scale):
    d = x_ref.shape[-1]
    mod = mod_ref[0]
    ng = ng_ref[...]
    h = _modnorm(x_ref[0], ng[0:1], mod[1:2], mod[0:1]).astype(BF16)
    q_ref[0] = (_dot(h, w_ref[:, :d]) * scale).astype(BF16)
    k_ref[0] = _dot(h, w_ref[:, d:2 * d]).astype(BF16)
    v_ref[0] = _dot(h, w_ref[:, 2 * d:]).astype(BF16)


def _na_kernel(q_ref, k_ref, v_ref, bias_ref, x_ref, mod_ref, ng_ref, wo_ref, o_ref,
               att_ref, s_ref, p_ref, *, rows):
    jb = pl.program_id(1)
    kh = min(NA_WIN_ROWS, rows)
    nkeys = kh * GRID_W
    npairs = NA_HEADS // 2
    pr = 2 * GRID_W
    lane_lo = lax.broadcasted_iota(jnp.int32, (GRID_W, LANES), 1) < LANES // 2

    def row_body(rr, carry):
        r = jb * 8 + rr
        start = jnp.clip(r - kh // 2, 0, rows - kh)
        dy0 = start - r + (NA_WIN_ROWS - 1)
        k0 = pl.multiple_of(start * GRID_W, GRID_W)
        q0 = pl.multiple_of(rr * GRID_W, GRID_W)
        for p in range(npairs):
            lanes = slice(p * LANES, (p + 1) * LANES)
            qp = q_ref[0, pl.ds(q0, GRID_W), lanes]
            zero = jnp.zeros_like(qp)
            qbd = jnp.concatenate([jnp.where(lane_lo, qp, zero), jnp.where(lane_lo, zero, qp)], axis=0)
            s = lax.dot_general(qbd, k_ref[0, pl.ds(k0, nkeys), lanes], (((1,), (1,)), ((), ())),
                                preferred_element_type=F32)
            bias = jnp.concatenate([bias_ref[p, dy0 + 2 * c] for c in range(kh // 2)], axis=1)
            s_ref[p * pr:(p + 1) * pr, :] = s + bias
        for p in range(npairs):
            s = s_ref[p * pr:(p + 1) * pr, :]
            p_ref[p * pr:(p + 1) * pr, :] = jnp.exp(s - jnp.max(s, axis=-1, keepdims=True)).astype(BF16)
        ones = jnp.ones((nkeys, LANES), BF16)
        for p in range(npairs):
            lanes = slice(p * LANES, (p + 1) * LANES)
            v_ones = jnp.concatenate([v_ref[0, pl.ds(k0, nkeys), lanes], ones], axis=1)
            ol = _dot(p_ref[p * pr:(p + 1) * pr, :], v_ones)
            o = ol[:, :LANES] * (1.0 / ol[:, LANES:])
            att_ref[pl.ds(q0, GRID_W), lanes] = jnp.where(lane_lo, o[:GRID_W], o[GRID_W:]).astype(BF16)
        return carry

    lax.fori_loop(0, 8, row_body, 0)
    mod = mod_ref[0]
    ng = ng_ref[...]
    m = _dot(att_ref[...], wo_ref[...])
    o_ref[0] = x_ref[0] + mod[2:3] * _rms(m, ng[1:2])


def _na_bias_kernel(rpb_ref, o_ref):
    p = pl.program_id(0)
    ndx = 2 * NA_WIN_COLS - 1
    shape = (2 * GRID_W, 2 * GRID_W)
    row = lax.broadcasted_iota(jnp.int32, shape, 0)
    lane = lax.broadcasted_iota(jnp.int32, shape, 1)
    q, kc = row & (GRID_W - 1), lane & (GRID_W - 1)
    top, left = row < GRID_W, lane < GRID_W
    col_start = jnp.clip(q - NA_WIN_COLS // 2, 0, GRID_W - NA_WIN_COLS)
    col_in = (kc >= col_start) & (kc < col_start + NA_WIN_COLS)
    dx = jnp.clip(kc - q, -(NA_WIN_COLS - 1), NA_WIN_COLS - 1) + (NA_WIN_COLS - 1)

    def one_dy(dy, prev):
        acc = jnp.full(shape, NEG_INF, F32)
        for k in range(ndx):
            val = jnp.where(top, rpb_ref[2 * p, dy, k], rpb_ref[2 * p + 1, dy, k])
            acc = jnp.where(dx == k, val, acc)
        acc = jnp.where(col_in, acc, NEG_INF)
        o_ref[0, jnp.maximum(dy - 1, 0)] = jnp.where(left, prev, acc)
        return acc

    lax.fori_loop(0, 2 * NA_WIN_ROWS - 1, one_dy, jnp.zeros(shape, F32))


def _na_bias_table(rpb):
    nh, ndy, _ = rpb.shape
    return pl.pallas_call(
        _na_bias_kernel,
        grid=(nh // 2,),
        in_specs=[pl.BlockSpec(memory_space=pltpu.SMEM)],
        out_specs=pl.BlockSpec((1, ndy - 1, 2 * GRID_W, 2 * GRID_W), lambda p: (p, 0, 0, 0)),
        out_shape=jax.ShapeDtypeStruct((nh // 2, ndy - 1, 2 * GRID_W, 2 * GRID_W), F32),
        compiler_params=_params("parallel"),
        name="na_bias",
    )(rpb)


def _na_layer(x, mod, ng, w_qkv, rpb, w_o, j):
    bsz, t, d = x.shape
    rows = t // GRID_W
    assert rows % 8 == 0
    tm = min(ROW_TILE, t)
    tile = pl.BlockSpec((1, tm, d), lambda b, i: (b, i, 0))
    modspec = pl.BlockSpec((1, 6, d), lambda b, i: (b, 0, 0))
    scale = (d // NA_HEADS) ** -0.5
    qkv_shape = jax.ShapeDtypeStruct((bsz, t, d), BF16)
    q, k, v = pl.pallas_call(
        functools.partial(_qkv_kernel, scale=scale),
        grid=(bsz, t // tm),
        in_specs=[tile, modspec, _resident(ng.shape), _resident(w_qkv.shape, j)],
        out_specs=[tile, tile, tile],
        out_shape=[qkv_shape] * 3,
        compiler_params=_params("parallel", "parallel"),
        name="na_qkv",
    )(x, mod, ng, w_qkv)
    bias = _na_bias_table(rpb[j])
    blk = 8 * GRID_W
    nkeys = min(NA_WIN_ROWS, rows) * GRID_W
    qtile = pl.BlockSpec((1, blk, d), lambda b, jb: (b, jb, 0))
    whole = pl.BlockSpec((1, t, d), lambda b, jb: (b, 0, 0), pipeline_mode=pl.Buffered(1))
    return pl.pallas_call(
        functools.partial(_na_kernel, rows=rows),
        grid=(bsz, rows // 8),
        in_specs=[qtile, whole, whole, _resident(bias.shape), qtile, modspec, _resident(ng.shape),
                  _resident(w_o.shape, j)],
        out_specs=qtile,
        out_shape=jax.ShapeDtypeStruct(x.shape, F32),
        scratch_shapes=[pltpu.VMEM((blk, d), BF16),
                        pltpu.VMEM((NA_HEADS * GRID_W, nkeys), F32),
                        pltpu.VMEM((NA_HEADS * GRID_W, nkeys), BF16)],
        compiler_params=_params("parallel", "arbitrary"),
        name="na_attn",
    )(q, k, v, bias, x, mod, ng, w_o)


def kernel(x, c, ada_w, ada_b, norm_g, ffn_w_gu, ffn_w_down, lru_w_in, lru_conv_w, lru_conv_b, lru_gate_w,
           lru_gate_b, lru_lambda, lru_w_out, na_w_qkv, na_rpb, na_w_o, sc_w_in, sc_conv_w, sc_w_out):
    depth = ada_w.shape[0]
    bsz, _, d = x.shape
    mods = _ada_mod(c, ada_w, ada_b).reshape(depth, bsz, 6, d)
    modts = jnp.transpose(mods, (0, 2, 1, 3))
    bf = lambda w: w.astype(BF16)
    ffn_w_gu, ffn_w_down = bf(ffn_w_gu), bf(ffn_w_down)
    lru_w_in, lru_w_out = bf(lru_w_in), bf(lru_w_out)
    na_w_qkv, na_w_o, sc_w_in, sc_w_out = bf(na_w_qkv), bf(na_w_o), bf(sc_w_in), bf(sc_w_out)
    time_major = False
    for i in range(depth):
        kind, j = i % 3, i // 3
        mod, modt, ng = mods[i], modts[i], norm_g[i]
        next_kind = (i + 1) % 3 if i + 1 < depth else 1
        if kind == 0:
            x = _lru_layer(x, mod, modt, ng, lru_w_in, lru_conv_w, lru_conv_b, lru_gate_w,
                           lru_gate_b, lru_lambda, lru_w_out, j, time_major)
        elif kind == 1:
            assert not time_major
            x = _na_layer(x, mod, ng, na_w_qkv, na_rpb, na_w_o, j)
        else:
            assert time_major
            x = _sc_layer(x, modt, ng, sc_w_in, sc_conv_w, sc_w_out, j)
        out_tm = next_kind != 1
        x = _ffn_layer(x, mod, modt, ng, ffn_w_gu, ffn_w_down, i, time_major, out_tm)
        time_major = out_tm
    return x
```
